```python
import jax
import jax.numpy as jnp
from jax import lax
import numpy as np

D_MODEL = 1024
BATCH = 16
SEQ = 2048
DEPTH = 1

GRID_W = 64
CTX_LEN = 256
N_HEADS_NA = 8
HEAD_DIM_NA = 64
WIN_H = 8
WIN_W = 16
NA_QCOLS = 16
NA_KCOLS = 32
N_HEADS_MLA = 8
MLA_Q_RANK = 768
MLA_KV_RANK = 256
MLA_NOPE_DIM = 64
MLA_ROPE_DIM = 32
MLA_V_DIM = 64
D_FF = 2816
ROPE_BASE = 10000.0
RMS_EPS = 1e-6
ATTN_QBLOCK = 128
N_MOD = 9
HALF_STEP = 0.5
NA_WIDTH = N_HEADS_NA * HEAD_DIM_NA
MLA_QK_DIM = MLA_NOPE_DIM + MLA_ROPE_DIM
IN_SPLITS = (NA_WIDTH, 2 * NA_WIDTH, 3 * NA_WIDTH, 3 * NA_WIDTH + MLA_Q_RANK,
             3 * NA_WIDTH + MLA_Q_RANK + MLA_KV_RANK,
             3 * NA_WIDTH + MLA_Q_RANK + MLA_KV_RANK + MLA_ROPE_DIM)
IN_COLS = IN_SPLITS[-1] + 2 * D_MODEL

kernel_name = 'hybrid_natten_mla_macaron_block'


def _rmsnorm(x, g):
    x32 = x.astype(jnp.float32)
    y = x32 * lax.rsqrt(jnp.mean(x32 * x32, axis=-1, keepdims=True) + RMS_EPS)
    return y.astype(x.dtype) * g


def _modnorm(x, g_pre, shift, scale):
    return _rmsnorm(x, g_pre) * (1 + scale) + shift


def _residual(x, y, g_post, gate, weight):
    return x + weight * gate * _rmsnorm(y, g_post)


def _swiglu(h, w1, w3, w2):
    return (jax.nn.silu(h @ w1) * (h @ w3)) @ w2


def _axial_angles(n_tokens):
    half = MLA_ROPE_DIM // 4
    freqs = ROPE_BASE ** (-jnp.arange(half, dtype=jnp.float32) / half)
    t = jnp.arange(n_tokens)
    rows = (t // GRID_W).astype(jnp.float32)
    cols = (t % GRID_W).astype(jnp.float32)
    return rows[:, None] * freqs, cols[:, None] * freqs


def _rope_axis(x, ang):
    half = x.shape[-1] // 2
    cos = jnp.cos(ang)[None, :, None, :].astype(x.dtype)
    sin = jnp.sin(ang)[None, :, None, :].astype(x.dtype)
    x1, x2 = x[..., :half], x[..., half:]
    return jnp.concatenate([x1 * cos - x2 * sin, x2 * cos + x1 * sin], axis=-1)


def _axial_rope(x, ang_r, ang_c):
    a = MLA_ROPE_DIM // 2
    return jnp.concatenate([_rope_axis(x[..., :a], ang_r), _rope_axis(x[..., a:], ang_c)], axis=-1)


def _project(h, w_in, b_gate, g_q, g_kv, w_uq, w_ukv):
    B, T, _ = h.shape
    qa, ka, va, cq, ckv, kr, gates = jnp.split(h @ w_in, IN_SPLITS, axis=-1)
    heads = lambda t, n: t.reshape(B, T, n, -1)
    q = heads(_rmsnorm(cq, g_q) @ w_uq, N_HEADS_MLA)
    kv = heads(_rmsnorm(ckv, g_kv) @ w_ukv, N_HEADS_MLA)
    g = jax.nn.sigmoid(gates + b_gate)
    return (heads(qa, N_HEADS_NA), heads(ka, N_HEADS_NA), heads(va, N_HEADS_NA),
            q[..., :MLA_NOPE_DIM], q[..., MLA_NOPE_DIM:],
            kv[..., :MLA_NOPE_DIM], kr[:, :, None, :], kv[..., MLA_NOPE_DIM:],
            g[..., :D_MODEL], g[..., D_MODEL:])


def _mla_q(q_nope, q_rope):
    return jnp.concatenate([q_nope, q_rope], axis=-1)


def _mla_k(k_nope, k_rope):
    k_rope = jnp.broadcast_to(k_rope, k_nope.shape[:-1] + (k_rope.shape[-1],))
    return jnp.concatenate([k_nope, k_rope], axis=-1)


def _attend(q, k, v):
    s = jnp.einsum('bqhd,bkhd->bhqk', q * q.shape[-1] ** -0.5, k).astype(jnp.float32)
    p = jax.nn.softmax(s, axis=-1).astype(v.dtype)
    return jnp.einsum('bhqk,bkhd->bqhd', p, v)


def _blocked_attention(q, k, v):
    B, S, H, dq = q.shape
    qb = q.reshape(B, S // ATTN_QBLOCK, ATTN_QBLOCK, H, dq).swapaxes(0, 1)
    out = lax.map(lambda qq: _attend(qq, k, v), qb)
    return out.swapaxes(0, 1).reshape(B, S, H, v.shape[-1])


def _na_column_tables():
    n_blk = GRID_W // NA_QCOLS
    j = np.arange(n_blk)
    k_start = np.clip(j * NA_QCOLS - WIN_W // 2, 0, GRID_W - NA_KCOLS)
    key_col = k_start[:, None] + np.arange(NA_KCOLS)
    q_col = j[:, None] * NA_QCOLS + np.arange(NA_QCOLS)
    w_start = np.clip(q_col - WIN_W // 2, 0, GRID_W - WIN_W)
    kc = key_col[:, None, :]
    valid = (kc >= w_start[..., None]) & (kc < w_start[..., None] + WIN_W)
    off = np.clip(kc - q_col[..., None], -(WIN_W - 1), WIN_W - 1) + (WIN_W - 1)
    return jnp.asarray(key_col, jnp.int32), jnp.asarray(valid), jnp.asarray(off, jnp.int32)


def _neighbourhood_attention(q, k, v, k_ctx, v_ctx, rpb, n_rows):
    B, S, H, d = q.shape
    kh = min(WIN_H, n_rows)
    n_blk = GRID_W // NA_QCOLS
    n_loc = kh * NA_KCOLS
    key_col, valid, col_off = _na_column_tables()
    grid = lambda t: t.reshape(B, n_rows, GRID_W, H, t.shape[-1])
    qg, kg, vg = grid(q * d ** -0.5), grid(k), grid(v)
    neg = jnp.finfo(jnp.float32).min

    def row(r):
        rs = jnp.clip(r - kh // 2, 0, n_rows - kh)
        q_r = lax.dynamic_index_in_dim(qg, r, axis=1, keepdims=False).reshape(B, n_blk, NA_QCOLS, H, d)
        k_b = jnp.moveaxis(lax.dynamic_slice_in_dim(kg, rs, kh, axis=1)[:, :, key_col], 2, 1)
        v_b = jnp.moveaxis(lax.dynamic_slice_in_dim(vg, rs, kh, axis=1)[:, :, key_col], 2, 1)
        row_off = rs + jnp.arange(kh) - r + (WIN_H - 1)
        bias = rpb[:, row_off[None, None, :, None], col_off[:, :, None, :]]
        s_loc = jnp.einsum('bjqhd,bjakhd->bhjqak', q_r, k_b).astype(jnp.float32) + bias.astype(jnp.float32)
        s_loc = jnp.where(valid[:, :, None, :], s_loc, neg).reshape(B, H, n_blk, NA_QCOLS, n_loc)
        s_ctx = jnp.einsum('bjqhd,bkhd->bhjqk', q_r, k_ctx).astype(jnp.float32)
        p = jax.nn.softmax(jnp.concatenate([s_loc, s_ctx], axis=-1), axis=-1).astype(v.dtype)
        p_loc = p[..., :n_loc].reshape(B, H, n_blk, NA_QCOLS, kh, NA_KCOLS)
        o = (jnp.einsum('bhjqak,bjakhd->bjqhd', p_loc, v_b)
             + jnp.einsum('bhjqk,bkhd->bjqhd', p[..., n_loc:], v_ctx))
        return o.reshape(B, GRID_W, H, v.shape[-1])

    out = lax.map(row, jnp.arange(n_rows))
    return jnp.moveaxis(out, 0, 1).reshape(B, S, H, v.shape[-1])


def _merge(o_na, o_mla, g_na, g_mla, w_o_na, w_o_mla, w_out):
    B, T = o_na.shape[:2]
    y = g_na * (o_na.reshape(B, T, -1) @ w_o_na) + g_mla * (o_mla.reshape(B, T, -1) @ w_o_mla)
    return y @ w_out


def setup_inputs(seed: int = 0) -> dict:
    key = jax.random.key(seed)
    ks = jax.random.split(key, 24)
    nrm = lambda k, shape, s: jax.random.normal(k, shape, jnp.float32) * s
    D = D_MODEL
    return {
        'x': nrm(ks[0], (BATCH, SEQ, D), 1.0),
        'c': nrm(ks[1], (BATCH, D), 1.0),
        'ctx': nrm(ks[2], (BATCH, CTX_LEN, D), 1.0),
        'c_ctx': nrm(ks[3], (D,), 1.0),
        'w_ada': nrm(ks[4], (DEPTH, D, N_MOD * D), 0.5 * D ** -0.5),
        'b_ada': nrm(ks[5], (DEPTH, N_MOD * D), 0.02),
        'norm_g': 1.0 + nrm(ks[6], (DEPTH, 6, D), 0.02),
        'ffn1_w1': nrm(ks[7], (DEPTH, D, D_FF), D ** -0.5),
        'ffn1_w3': nrm(ks[8], (DEPTH, D, D_FF), D ** -0.5),
        'ffn1_w2': nrm(ks[9], (DEPTH, D_FF, D), D_FF ** -0.5),
        'w_in': nrm(ks[10], (DEPTH, D, IN_COLS), D ** -0.5),
        'b_gate': nrm(ks[11], (DEPTH, 2 * D), 0.1),
        'g_q_lora': 1.0 + nrm(ks[12], (DEPTH, MLA_Q_RANK), 0.02),
        'g_kv_lora': 1.0 + nrm(ks[13], (DEPTH, MLA_KV_RANK), 0.02),
        'w_uq': nrm(ks[14], (DEPTH, MLA_Q_RANK, N_HEADS_MLA * MLA_QK_DIM), MLA_Q_RANK ** -0.5),
        'w_ukv': nrm(ks[15], (DEPTH, MLA_KV_RANK, N_HEADS_MLA * (MLA_NOPE_DIM + MLA_V_DIM)), MLA_KV_RANK ** -0.5),
        'rpb': nrm(ks[16], (DEPTH, N_HEADS_NA, 2 * WIN_H - 1, 2 * WIN_W - 1), 0.1),
        'w_o_na': nrm(ks[17], (DEPTH, NA_WIDTH, D), NA_WIDTH ** -0.5),
        'w_o_mla': nrm(ks[18], (DEPTH, N_HEADS_MLA * MLA_V_DIM, D), (N_HEADS_MLA * MLA_V_DIM) ** -0.5),
        'w_out': nrm(ks[19], (DEPTH, D, D), D ** -0.5),
        'ffn2_w1': nrm(ks[20], (DEPTH, D, D_FF), D ** -0.5),
        'ffn2_w3': nrm(ks[21], (DEPTH, D, D_FF), D ** -0.5),
        'ffn2_w2': nrm(ks[22], (DEPTH, D_FF, D), D_FF ** -0.5),
    }


def reference(x, c, ctx, c_ctx, w_ada, b_ada, norm_g, ffn1_w1, ffn1_w3, ffn1_w2, w_in, b_gate,
              g_q_lora, g_kv_lora, w_uq, w_ukv, rpb, w_o_na, w_o_mla, w_out, ffn2_w1, ffn2_w3, ffn2_w2):
    n_lat = x.shape[1]
    n_rows = n_lat // GRID_W
    ang_r, ang_c = _axial_angles(n_lat)
    h_ctx = ctx
    for l in range(DEPTH):
        last = l == DEPTH - 1
        m = jnp.split((jax.nn.silu(c) @ w_ada[l] + b_ada[l])[:, None, :], N_MOD, axis=-1)
        mc = jnp.split(jax.nn.silu(c_ctx) @ w_ada[l] + b_ada[l], N_MOD, axis=-1)
        g = norm_g[l]
        x = _residual(x, _swiglu(_modnorm(x, g[0], m[0], m[1]), ffn1_w1[l], ffn1_w3[l], ffn1_w2[l]),
                      g[1], m[2], HALF_STEP)
        h_ctx = _residual(h_ctx, _swiglu(_modnorm(h_ctx, g[0], mc[0], mc[1]), ffn1_w1[l], ffn1_w3[l], ffn1_w2[l]),
                          g[1], mc[2], HALF_STEP)
        pw = (w_in[l], b_gate[l], g_q_lora[l], g_kv_lora[l], w_uq[l], w_ukv[l])
        qa, ka, va, qn, qr, kn, kr, vb, ga, gb = _project(_modnorm(x, g[2], m[3], m[4]), *pw)
        cqa, cka, cva, cqn, cqr, ckn, ckr, cvb, cga, cgb = _project(_modnorm(h_ctx, g[2], mc[3], mc[4]), *pw)
        o_na = _neighbourhood_attention(qa, ka, va, cka, cva, rpb[l], n_rows)
        k_ctx_mla = _mla_k(ckn, ckr)
        q_mla = _mla_q(qn, _axial_rope(qr, ang_r, ang_c))
        k_mla = jnp.concatenate([k_ctx_mla, _mla_k(kn, _axial_rope(kr, ang_r, ang_c))], axis=1)
        v_mla = jnp.concatenate([cvb, vb], axis=1)
        o_mla = _blocked_attention(q_mla, k_mla, v_mla)
        y = _merge(o_na, o_mla, ga, gb, w_o_na[l], w_o_mla[l], w_out[l])
        x = _residual(x, y, g[3], m[5], 1.0)
        if not last:
            yc = _merge(_attend(cqa, cka, cva), _attend(_mla_q(cqn, cqr), k_ctx_mla, cvb),
                        cga, cgb, w_o_na[l], w_o_mla[l], w_out[l])
            h_ctx = _residual(h_ctx, yc, g[3], mc[5], 1.0)
            h_ctx = _residual(h_ctx, _swiglu(_modnorm(h_ctx, g[4], mc[6], mc[7]), ffn2_w1[l], ffn2_w3[l], ffn2_w2[l]),
                              g[5], mc[8], HALF_STEP)
        x = _residual(x, _swiglu(_modnorm(x, g[4], m[6], m[7]), ffn2_w1[l], ffn2_w3[l], ffn2_w2[l]),
                      g[5], m[8], HALF_STEP)
    return x
```

```python
import functools

import jax
import jax.numpy as jnp
import numpy as np
from jax import lax
from jax.experimental import pallas as pl
from jax.experimental.pallas import tpu as pltpu

D_MODEL = 1024
GRID_W = 64
CTX_LEN = 256
N_HEADS = 8
NA_DIM = 64
NA_WIDTH = N_HEADS * NA_DIM
WIN_H = 8
WIN_W = 16
MLA_Q_RANK = 768
MLA_KV_RANK = 256
MLA_NOPE = 64
MLA_ROPE = 32
MLA_V = 64
MLA_QK = MLA_NOPE + MLA_ROPE
D_FF = 2816
ROPE_BASE = 10000.0
RMS_EPS = 1e-6
N_MOD = 9
HALF_STEP = 0.5

LANES = 128
MLA_PAD = N_HEADS * LANES
NA_QROWS = 4
NA_KROWS = 12
NA_GROUPS_CASES = 3
VMEM_LIMIT = 56 * 1024 * 1024

BF = jnp.bfloat16
F32 = jnp.float32
NEG = float(np.finfo(np.float32).min)


def _dot(a, b):
    return jnp.dot(a, b, preferred_element_type=F32)


def _dot_nt(a, b):
    return lax.dot_general(a, b, (((1,), (1,)), ((), ())), preferred_element_type=F32)


def _rms(x):
    return x * lax.rsqrt(jnp.mean(x * x, axis=-1, keepdims=True) + RMS_EPS)


def _silu(x):
    return x * jax.nn.sigmoid(x)


def _const_spec(shape):
    nd = len(shape)
    return pl.BlockSpec(shape, lambda *_: (0,) * nd, pipeline_mode=pl.Buffered(1))


def _params(n_axes):
    return pltpu.CompilerParams(dimension_semantics=("arbitrary",) * n_axes,
                                vmem_limit_bytes=VMEM_LIMIT)


def _ada_body(c_ref, w_ref, b_ref, o_ref):
    o_ref[0] = _dot(_silu(c_ref[...]).astype(BF), w_ref[...].astype(BF)) + b_ref[...]


def _ada(cc, w_ada, b_ada):
    rows = cc.shape[0]
    return pl.pallas_call(
        _ada_body,
        grid=(N_MOD,),
        in_specs=[pl.BlockSpec((rows, D_MODEL), lambda j: (0, 0)),
                  pl.BlockSpec((D_MODEL, D_MODEL), lambda j: (0, j)),
                  pl.BlockSpec((1, D_MODEL), lambda j: (0, j))],
        out_specs=pl.BlockSpec((1, rows, D_MODEL), lambda j: (j, 0, 0)),
        out_shape=jax.ShapeDtypeStruct((N_MOD, rows, D_MODEL), F32),
        compiler_params=_params(1),
        name="ada",
    )(cc, w_ada, b_ada)


FF_CHUNKS = ((0, 1536), (1536, D_FF))


def _ffn_body(x_ref, mod_ref, g_ref, w1_ref, w3_ref, w2_ref, o_ref, *, km, kg):
    x = x_ref[...]
    mod = mod_ref[0]
    g = g_ref[...]
    h = (_rms(x) * g[kg:kg + 1] * (1.0 + mod[km + 1:km + 2]) + mod[km:km + 1]).astype(BF)
    y = None
    for lo, hi in FF_CHUNKS:
        a = _dot(h, w1_ref[:, lo:hi])
        b = _dot(h, w3_ref[:, lo:hi])
        part = _dot((_silu(a) * b).astype(BF), w2_ref[lo:hi, :])
        y = part if y is None else y + part
    o_ref[...] = x + (HALF_STEP * mod[km + 2:km + 3]) * (_rms(y) * g[kg + 1:kg + 2])


def _ffn(x2d, mods, g, w1, w3, w2, *, km, kg, mod_index, tm):
    rows = x2d.shape[0]
    return pl.pallas_call(
        functools.partial(_ffn_body, km=km, kg=kg),
        grid=(rows // tm,),
        in_specs=[pl.BlockSpec((tm, D_MODEL), lambda i: (i, 0)),
                  pl.BlockSpec((1, N_MOD, D_MODEL), lambda i: (mod_index(i), 0, 0)),
                  _const_spec(g.shape),
                  _const_spec(w1.shape), _const_spec(w3.shape), _const_spec(w2.shape)],
        out_specs=pl.BlockSpec((tm, D_MODEL), lambda i: (i, 0)),
        out_shape=jax.ShapeDtypeStruct(x2d.shape, F32),
        compiler_params=_params(1),
        name="ffn",
    )(x2d, mods, g, w1, w3, w2)


def _proj_body(*refs, latent):
    if latent:
        (x_ref, mod_ref, g_ref, wa_ref, wkr_ref, wg_ref, bg_ref, gq_ref, gkv_ref, wuq_ref, wukv_ref,
         cosq_ref, sinq_ref, cosk_ref, sink_ref,
         qa_ref, ka_ref, va_ref, qm_ref, km_ref, vm_ref, ga_ref, gb_ref) = refs
    else:
        (x_ref, mod_ref, g_ref, wa_ref, wkr_ref, gkv_ref, wukv_ref, ka_ref, va_ref, km_ref, vm_ref) = refs
    mod = mod_ref[0]
    g = g_ref[...]
    h = (_rms(x_ref[...]) * g[2:3] * (1.0 + mod[4:5]) + mod[3:4]).astype(BF)
    u = _dot(h, wa_ref[...])
    if latent:
        qa_ref[...] = (u[:, :NA_WIDTH] * (NA_DIM ** -0.5)).astype(BF)
        u = u[:, NA_WIDTH:]
    ka_ref[...] = u[:, :NA_WIDTH].astype(BF)
    va_ref[...] = u[:, NA_WIDTH:2 * NA_WIDTH].astype(BF)
    u = u[:, 2 * NA_WIDTH:]
    if latent:
        cq = u[:, :MLA_Q_RANK]
        u = u[:, MLA_Q_RANK:]
        q2 = _dot((_rms(cq) * gq_ref[...]).astype(BF), wuq_ref[...])
        cosq, sinq = cosq_ref[...], sinq_ref[...]
        for hd in range(N_HEADS):
            sl = slice(hd * LANES, (hd + 1) * LANES)
            sw = slice(MLA_PAD + hd * LANES, MLA_PAD + (hd + 1) * LANES)
            qm_ref[:, sl] = (q2[:, sl] * cosq + q2[:, sw] * sinq).astype(BF)
        gates = jax.nn.sigmoid(_dot(h, wg_ref[...]) + bg_ref[...])
        ga_ref[...] = gates[:, :D_MODEL].astype(BF)
        gb_ref[...] = gates[:, D_MODEL:].astype(BF)
    kv = _dot((_rms(u) * gkv_ref[...]).astype(BF), wukv_ref[...])
    vm_ref[...] = kv[:, MLA_PAD:].astype(BF)
    kr2 = _dot(h, wkr_ref[...])
    if latent:
        kr = kr2[:, :LANES] * cosk_ref[...] + kr2[:, LANES:] * sink_ref[...]
    else:
        kr = kr2[:, :LANES]
    for hd in range(N_HEADS):
        sl = slice(hd * LANES, (hd + 1) * LANES)
        km_ref[:, sl] = (kv[:, sl] + kr).astype(BF)


def _proj(x2d, mods, g, weights, tables, *, latent, mod_index, tm, seq):
    rows = x2d.shape[0]
    row_spec = lambda w: pl.BlockSpec((tm, w), lambda i: (i, 0))
    ins = [x2d, mods, g] + list(weights)
    in_specs = [row_spec(D_MODEL),
                pl.BlockSpec((1, N_MOD, D_MODEL), lambda i: (mod_index(i), 0, 0)),
                _const_spec(g.shape)] + [_const_spec(w.shape) for w in weights]
    if latent:
        ins += list(tables)
        in_specs += [pl.BlockSpec((tm, LANES), lambda i: (i % (seq // tm), 0)) for _ in tables]
        widths = (NA_WIDTH, NA_WIDTH, NA_WIDTH, MLA_PAD, MLA_PAD, N_HEADS * MLA_V, D_MODEL, D_MODEL)
    else:
        widths = (NA_WIDTH, NA_WIDTH, MLA_PAD, N_HEADS * MLA_V)
    return pl.pallas_call(
        functools.partial(_proj_body, latent=latent),
        grid=(rows // tm,),
        in_specs=in_specs,
        out_specs=[row_spec(w) for w in widths],
        out_shape=[jax.ShapeDtypeStruct((rows, w), BF) for w in widths],
        compiler_params=_params(1),
        name="proj_lat" if latent else "proj_ctx",
    )(*ins)


def _na_geometry(case, i, j):
    if case == 0:
        return j < WIN_H, j - i
    if case == 1:
        return i <= j < i + WIN_H, j - i - WIN_H // 2
    return NA_KROWS - WIN_H <= j, j - i - (NA_KROWS - NA_QROWS)


def _na_bias_body(rpb_ref, o_ref):
    hd = pl.program_id(0)
    qc = lax.broadcasted_iota(jnp.int32, (GRID_W, GRID_W), 0)
    kc = lax.broadcasted_iota(jnp.int32, (GRID_W, GRID_W), 1)
    dcol = kc - qc
    ws = jnp.clip(qc - WIN_W // 2, 0, GRID_W - WIN_W)
    col_ok = (kc >= ws) & (kc < ws + WIN_W)
    n_dr, n_dc = 2 * WIN_H - 1, 2 * WIN_W - 1
    blocks = {}
    for case in range(NA_GROUPS_CASES):
        for i in range(NA_QROWS):
            for j in range(NA_KROWS):
                ok, dr = _na_geometry(case, i, j)
                blocks.setdefault(dr if ok else None, []).append((case, i, j))
    neg = jnp.full((GRID_W, GRID_W), NEG, F32)
    for dr, places in blocks.items():
        if dr is None:
            t = neg
        else:
            t = jnp.zeros((GRID_W, GRID_W), F32)
            base = hd * (n_dr * n_dc) + (dr + WIN_H - 1) * n_dc
            for d in range(n_dc):
                t = jnp.where(dcol == d - (WIN_W - 1), rpb_ref[base + d], t)
            t = jnp.where(col_ok, t, NEG)
        for case, i, j in places:
            o_ref[case, 0, i * GRID_W:(i + 1) * GRID_W, j * GRID_W:(j + 1) * GRID_W] = t


def _na_bias(rpb):
    nq, nk = NA_QROWS * GRID_W, NA_KROWS * GRID_W
    return pl.pallas_call(
        _na_bias_body,
        grid=(N_HEADS,),
        in_specs=[pl.BlockSpec(memory_space=pltpu.SMEM)],
        out_specs=pl.BlockSpec((NA_GROUPS_CASES, 1, nq, nk), lambda h: (0, h, 0, 0)),
        out_shape=jax.ShapeDtypeStruct((NA_GROUPS_CASES, N_HEADS, nq, nk), F32),
        compiler_params=_params(1),
        name="na_bias",
    )(rpb.reshape(-1))


def _softmax_pv(s_list, v_list):
    m = functools.reduce(jnp.maximum, [jnp.max(s, axis=-1, keepdims=True) for s in s_list])
    p_list = [jnp.exp(s - m) for s in s_list]
    l = functools.reduce(lambda a, b: a + b, [jnp.sum(p, axis=-1, keepdims=True) for p in p_list])
    o = functools.reduce(lambda a, b: a + b, [_dot(p.astype(BF), v) for p, v in zip(p_list, v_list)])
    return o / l


def _na_body(q_ref, k_ref, v_ref, ck_ref, cv_ref, tab_ref, o_ref):
    grp = pl.program_id(1)
    n_rows = k_ref.shape[1] // GRID_W
    k_row0 = jnp.clip(grp * NA_QROWS - WIN_H // 2, 0, n_rows - NA_KROWS)
    start = pl.multiple_of(k_row0 * GRID_W, GRID_W)
    nk = NA_KROWS * GRID_W
    low = lax.broadcasted_iota(jnp.int32, (1, LANES), 1) < NA_DIM
    for pair in range(N_HEADS // 2):
        sl = slice(pair * LANES, (pair + 1) * LANES)
        q2 = q_ref[0, :, sl]
        k2 = k_ref[0, pl.ds(start, nk), sl]
        v2 = v_ref[0, pl.ds(start, nk), sl]
        ck2 = ck_ref[0, :, sl]
        cv2 = cv_ref[0, :, sl]
        res = []
        for half in range(2):
            qh = jnp.where(low if half == 0 else ~low, q2, jnp.zeros_like(q2))
            s_loc = _dot_nt(qh, k2) + tab_ref[0, 2 * pair + half]
            s_ctx = _dot_nt(qh, ck2)
            res.append(_softmax_pv([s_loc, s_ctx], [v2, cv2]))
        o_ref[0, :, sl] = jnp.where(low, res[0], res[1]).astype(BF)


def _na(qa, ka, va, cka, cva, table, *, batch, seq):
    nq = NA_QROWS * GRID_W
    n_groups = seq // nq
    case = lambda g: (g + n_groups - 2) // (n_groups - 1) + g // (n_groups - 1)
    tile = pl.BlockSpec((1, nq, NA_WIDTH), lambda b, g: (b, g, 0))
    lat = pl.BlockSpec((1, seq, NA_WIDTH), lambda b, g: (b, 0, 0))
    ctx = pl.BlockSpec((1, CTX_LEN, NA_WIDTH), lambda b, g: (b, 0, 0))
    return pl.pallas_call(
        _na_body,
        grid=(batch, n_groups),
        in_specs=[tile, lat, lat, ctx, ctx,
                  pl.BlockSpec((1, N_HEADS, nq, NA_KROWS * GRID_W), lambda b, g: (case(g), 0, 0, 0))],
        out_specs=tile,
        out_shape=jax.ShapeDtypeStruct((batch, seq, NA_WIDTH), BF),
        compiler_params=_params(2),
        name="na",
    )(qa, ka, va, cka, cva, table)


def _mla_body(q_ref, k_ref, ck_ref, v_ref, cv_ref, o_ref):
    low = lax.broadcasted_iota(jnp.int32, (1, LANES), 1) < MLA_V
    for pair in range(N_HEADS // 2):
        sl = slice(pair * LANES, (pair + 1) * LANES)
        v2 = v_ref[0, :, sl]
        cv2 = cv_ref[0, :, sl]
        res = []
        for half in range(2):
            hs = slice((2 * pair + half) * LANES, (2 * pair + half + 1) * LANES)
            qh = q_ref[0, :, hs]
            s_ctx = _dot_nt(qh, ck_ref[0, :, hs])
            s_lat = _dot_nt(qh, k_ref[0, :, hs])
            res.append(_softmax_pv([s_ctx, s_lat], [cv2, v2]))
        o_ref[0, :, sl] = jnp.where(low, res[0], res[1]).astype(BF)


def _mla(qm, km, ckm, vm, cvm, *, batch, seq, tq):
    vw = N_HEADS * MLA_V
    return pl.pallas_call(
        _mla_body,
        grid=(batch, seq // tq),
        in_specs=[pl.BlockSpec((1, tq, MLA_PAD), lambda b, i: (b, i, 0)),
                  pl.BlockSpec((1, seq, MLA_PAD), lambda b, i: (b, 0, 0)),
                  pl.BlockSpec((1, CTX_LEN, MLA_PAD), lambda b, i: (b, 0, 0)),
                  pl.BlockSpec((1, seq, vw), lambda b, i: (b, 0, 0)),
                  pl.BlockSpec((1, CTX_LEN, vw), lambda b, i: (b, 0, 0))],
        out_specs=pl.BlockSpec((1, tq, vw), lambda b, i: (b, i, 0)),
        out_shape=jax.ShapeDtypeStruct((batch, seq, vw), BF),
        compiler_params=_params(2),
        name="mla",
    )(qm, km, ckm, vm, cvm)


def _merge_body(x_ref, ona_ref, omla_ref, ga_ref, gb_ref, mod_ref, g_ref, wna_ref, wmla_ref, wout_ref, o_ref):
    mod = mod_ref[0]
    g = g_ref[...]
    y = (ga_ref[...].astype(F32) * _dot(ona_ref[...], wna_ref[...])
         + gb_ref[...].astype(F32) * _dot(omla_ref[...], wmla_ref[...]))
    y = _dot(y.astype(BF), wout_ref[...])
    o_ref[...] = x_ref[...] + mod[5:6] * (_rms(y) * g[3:4])


def _merge(x2d, ona, omla, ga, gb, mods, g, wna, wmla, wout, *, mod_index, tm):
    rows = x2d.shape[0]
    row_spec = lambda w: pl.BlockSpec((tm, w), lambda i: (i, 0))
    return pl.pallas_call(
        _merge_body,
        grid=(rows // tm,),
        in_specs=[row_spec(D_MODEL), row_spec(NA_WIDTH), row_spec(N_HEADS * MLA_V),
                  row_spec(D_MODEL), row_spec(D_MODEL),
                  pl.BlockSpec((1, N_MOD, D_MODEL), lambda i: (mod_index(i), 0, 0)),
                  _const_spec(g.shape), _const_spec(wna.shape), _const_spec(wmla.shape),
                  _const_spec(wout.shape)],
        out_specs=row_spec(D_MODEL),
        out_shape=jax.ShapeDtypeStruct(x2d.shape, F32),
        compiler_params=_params(1),
        name="merge",
    )(x2d, ona, omla, ga, gb, mods, g, wna, wmla, wout)


ROPE_SWAP = np.concatenate([np.arange(8, 16), np.arange(0, 8), np.arange(24, 32), np.arange(16, 24)])


def _rope_tables(seq):
    n_freq = MLA_ROPE // 4
    freqs = ROPE_BASE ** (-jnp.arange(n_freq, dtype=F32) / n_freq)
    t = jnp.arange(seq)
    ang_r = (t // GRID_W).astype(F32)[:, None] * freqs
    ang_c = (t % GRID_W).astype(F32)[:, None] * freqs
    cos = jnp.concatenate([jnp.cos(ang_r)] * 2 + [jnp.cos(ang_c)] * 2, axis=-1)
    sin = jnp.concatenate([-jnp.sin(ang_r), jnp.sin(ang_r), -jnp.sin(ang_c), jnp.sin(ang_c)], axis=-1)
    scale = MLA_QK ** -0.5
    zeros = lambda w: jnp.zeros((seq, w), F32)
    tail = LANES - MLA_QK
    cosq = jnp.concatenate([jnp.full((seq, MLA_NOPE), scale, F32), scale * cos, zeros(tail)], axis=-1)
    sinq = jnp.concatenate([zeros(MLA_NOPE), scale * sin, zeros(tail)], axis=-1)
    cosk = jnp.concatenate([zeros(MLA_NOPE), cos, zeros(tail)], axis=-1)
    sink = jnp.concatenate([zeros(MLA_NOPE), sin, zeros(tail)], axis=-1)
    return cosq, sinq, cosk, sink


def _layer_weights(w_in, w_uq, w_ukv):
    o_cq = 3 * NA_WIDTH
    o_ckv = o_cq + MLA_Q_RANK
    o_kr = o_ckv + MLA_KV_RANK
    o_g = o_kr + MLA_ROPE
    wa_lat = w_in[:, :o_kr].astype(BF)
    wa_ctx = jnp.concatenate([w_in[:, NA_WIDTH:o_cq], w_in[:, o_ckv:o_kr]], axis=1).astype(BF)
    wkr = w_in[:, o_kr:o_g]
    wkr2 = jnp.zeros((D_MODEL, 2 * LANES), F32)
    wkr2 = wkr2.at[:, MLA_NOPE:MLA_QK].set(wkr).at[:, LANES + MLA_NOPE:LANES + MLA_QK].set(wkr[:, ROPE_SWAP])
    wg = w_in[:, o_g:].astype(BF)
    uq = w_uq.reshape(MLA_Q_RANK, N_HEADS, MLA_QK)
    uq_plain = jnp.pad(uq, ((0, 0), (0, 0), (0, LANES - MLA_QK)))
    uq_swap = jnp.pad(uq[:, :, MLA_NOPE:][:, :, ROPE_SWAP], ((0, 0), (0, 0), (MLA_NOPE, LANES - MLA_QK)))
    wuq = jnp.concatenate([uq_plain.reshape(MLA_Q_RANK, MLA_PAD), uq_swap.reshape(MLA_Q_RANK, MLA_PAD)], axis=1)
    ukv = w_ukv.reshape(MLA_KV_RANK, N_HEADS, MLA_NOPE + MLA_V)
    uk = jnp.pad(ukv[:, :, :MLA_NOPE], ((0, 0), (0, 0), (0, LANES - MLA_NOPE))).reshape(MLA_KV_RANK, MLA_PAD)
    uv = ukv[:, :, MLA_NOPE:].reshape(MLA_KV_RANK, N_HEADS * MLA_V)
    wukv = jnp.concatenate([uk, uv], axis=1)
    return wa_lat, wa_ctx, wkr2.astype(BF), wg, wuq.astype(BF), wukv.astype(BF)


def kernel(x, c, ctx, c_ctx, w_ada, b_ada, norm_g, ffn1_w1, ffn1_w3, ffn1_w2, w_in, b_gate, g_q_lora, g_kv_lora,
           w_uq, w_ukv, rpb, w_o_na, w_o_mla, w_out, ffn2_w1, ffn2_w3, ffn2_w2):
    batch, seq, d = x.shape
    n_ctx = ctx.shape[1]
    depth = w_ada.shape[0]
    assert d == D_MODEL and n_ctx == CTX_LEN and seq % (NA_QROWS * GRID_W) == 0
    assert seq // GRID_W >= NA_KROWS

    mod_rows = -(-(batch + 1) // 16) * 16
    ctx_mod = batch
    tm = 512
    lat_mod = lambda i: i // (seq // tm)
    ctx_mod_index = lambda i: ctx_mod
    tables = _rope_tables(seq)

    xl = x.reshape(batch * seq, d)
    xc = ctx.reshape(batch * n_ctx, d)
    for l in range(depth):
        last = l == depth - 1
        cc = jnp.zeros((mod_rows, d), F32).at[:batch].set(c).at[batch].set(c_ctx)
        mods = jnp.transpose(_ada(cc, w_ada[l], b_ada[l][None, :]), (1, 0, 2))
        g = norm_g[l]
        f1 = (ffn1_w1[l].astype(BF), ffn1_w3[l].astype(BF), ffn1_w2[l].astype(BF))
        f2 = (ffn2_w1[l].astype(BF), ffn2_w3[l].astype(BF), ffn2_w2[l].astype(BF))
        wa_lat, wa_ctx, wkr2, wg, wuq, wukv = _layer_weights(w_in[l], w_uq[l], w_ukv[l])

        xl = _ffn(xl, mods, g, *f1, km=0, kg=0, mod_index=lat_mod, tm=tm)
        xc = _ffn(xc, mods, g, *f1, km=0, kg=0, mod_index=ctx_mod_index, tm=tm)

        lat_w = (wa_lat, wkr2, wg, b_gate[l][None, :], g_q_lora[l][None, :], g_kv_lora[l][None, :], wuq, wukv)
        qa, ka, va, qm, km_, vm, ga, gb = _proj(xl, mods, g, lat_w, tables, latent=True, mod_index=lat_mod,
                                                tm=tm, seq=seq)
        ctx_w = (wa_ctx, wkr2, g_kv_lora[l][None, :], wukv)
        cka, cva, ckm, cvm = _proj(xc, mods, g, ctx_w, (), latent=False, mod_index=ctx_mod_index, tm=tm, seq=seq)

        b3 = lambda t, n: t.reshape(batch, n, t.shape[-1])
        o_na = _na(b3(qa, seq), b3(ka, seq), b3(va, seq), b3(cka, n_ctx), b3(cva, n_ctx), _na_bias(rpb[l]),
                   batch=batch, seq=seq)
        o_mla = _mla(b3(qm, seq), b3(km_, seq), b3(ckm, n_ctx), b3(vm, seq), b3(cvm, n_ctx),
                     batch=batch, seq=seq, tq=256)
        xl = _merge(xl, o_na.reshape(batch * seq, -1), o_mla.reshape(batch * seq, -1), ga, gb, mods, g,
                    w_o_na[l].astype(BF), w_o_mla[l].astype(BF), w_out[l].astype(BF), mod_index=lat_mod, tm=tm)
        if not last:
            raise NotImplementedError("context-stream mixer update is only needed for depth > 1")
        xl = _ffn(xl, mods, g, *f2, km=6, kg=4, mod_index=lat_mod, tm=tm)
    return xl.reshape(batch, seq, d)
```

```python
import functools

import jax
import jax.numpy as jnp
import numpy as np
from jax import lax
from jax.experimental import pallas as pl
from jax.experimental.pallas import tpu as pltpu

D_MODEL = 1024
GRID_W = 64
CTX_LEN = 256
N_HEADS = 8
NA_DIM = 64
NA_WIDTH = N_HEADS * NA_DIM
WIN_H = 8
WIN_W = 16
MLA_Q_RANK = 768
MLA_KV_RANK = 256
MLA_NOPE = 64
MLA_ROPE = 32
MLA_V = 64
MLA_QK = MLA_NOPE + MLA_ROPE
MLA_VW = N_HEADS * MLA_V
D_FF = 2816
ROPE_BASE = 10000.0
RMS_EPS = 1e-6
N_MOD = 9
HALF_STEP = 0.5
LOG2E = float(np.log2(np.e))

LANES = 128
MLA_PAD = N_HEADS * LANES
NA_QROWS = 4
NA_KROWS = 12
NA_CASES = 3
VMEM_LIMIT = 56 * 1024 * 1024

BF = jnp.bfloat16
F32 = jnp.float32
NEG = float(np.finfo(np.float32).min)


def _dot(a, b):
    return jnp.dot(a, b, preferred_element_type=F32)


def _dot_nt(a, b):
    return lax.dot_general(a, b, (((1,), (1,)), ((), ())), preferred_element_type=F32)


def _rms(x):
    return x * lax.rsqrt(jnp.mean(x * x, axis=-1, keepdims=True) + RMS_EPS)


def _silu(x):
    return x * jax.nn.sigmoid(x)


def _const_spec(shape):
    nd = len(shape)
    return pl.BlockSpec(shape, lambda *_: (0,) * nd, pipeline_mode=pl.Buffered(1))


def _params(n_axes):
    return pltpu.CompilerParams(dimension_semantics=("arbitrary",) * n_axes,
                                vmem_limit_bytes=VMEM_LIMIT)


def _ada_body(c_ref, w_ref, b_ref, o_ref):
    o_ref[0] = _dot(_silu(c_ref[...]).astype(BF), w_ref[...].astype(BF)) + b_ref[...]


def _ada(cc, w_ada, b_ada):
    rows = cc.shape[0]
    return pl.pallas_call(
        _ada_body,
        grid=(N_MOD,),
        in_specs=[pl.BlockSpec((rows, D_MODEL), lambda j: (0, 0)),
                  pl.BlockSpec((D_MODEL, D_MODEL), lambda j: (0, j)),
                  pl.BlockSpec((1, D_MODEL), lambda j: (0, j))],
        out_specs=pl.BlockSpec((1, rows, D_MODEL), lambda j: (j, 0, 0)),
        out_shape=jax.ShapeDtypeStruct((N_MOD, rows, D_MODEL), F32),
        compiler_params=_params(1),
        name="ada",
    )(cc, w_ada, b_ada)


FF_CHUNKS = ((0, 1536), (1536, D_FF))


def _ffn_body(x_ref, mod_ref, g_ref, w1_ref, w3_ref, w2_ref, o_ref, *, km, kg):
    x = x_ref[...]
    mod = mod_ref[0]
    g = g_ref[...]
    h = (_rms(x) * g[kg:kg + 1] * (1.0 + mod[km + 1:km + 2]) + mod[km:km + 1]).astype(BF)
    y = None
    for lo, hi in FF_CHUNKS:
        a = _dot(h, w1_ref[:, lo:hi])
        b = _dot(h, w3_ref[:, lo:hi])
        part = _dot((_silu(a) * b).astype(BF), w2_ref[lo:hi, :])
        y = part if y is None else y + part
    o_ref[...] = x + (HALF_STEP * mod[km + 2:km + 3]) * (_rms(y) * g[kg + 1:kg + 2])


def _ffn(x2d, mods, g, w1, w3, w2, *, km, kg, mod_index, tm):
    rows = x2d.shape[0]
    return pl.pallas_call(
        functools.partial(_ffn_body, km=km, kg=kg),
        grid=(rows // tm,),
        in_specs=[pl.BlockSpec((tm, D_MODEL), lambda i: (i, 0)),
                  pl.BlockSpec((1, N_MOD, D_MODEL), lambda i: (mod_index(i), 0, 0)),
                  _const_spec(g.shape),
                  _const_spec(w1.shape), _const_spec(w3.shape), _const_spec(w2.shape)],
        out_specs=pl.BlockSpec((tm, D_MODEL), lambda i: (i, 0)),
        out_shape=jax.ShapeDtypeStruct(x2d.shape, F32),
        compiler_params=_params(1),
        name="ffn",
    )(x2d, mods, g, w1, w3, w2)


def _proj_body(*refs, latent):
    if latent:
        (x_ref, mod_ref, g_ref, wa_ref, wvat_ref, wkr_ref, wuk_ref, wuvt_ref, gkv_ref,
         wg_ref, bg_ref, gq_ref, wuq_ref, cosq_ref, sinq_ref, cosk_ref, sink_ref,
         ka_ref, vat_ref, km_ref, vmt_ref, qa_ref, qm_ref, ga_ref, gb_ref) = refs
    else:
        (x_ref, mod_ref, g_ref, wa_ref, wvat_ref, wkr_ref, wuk_ref, wuvt_ref, gkv_ref,
         ka_ref, vat_ref, km_ref, vmt_ref) = refs
    mod = mod_ref[0]
    g = g_ref[...]
    h = (_rms(x_ref[...]) * g[2:3] * (1.0 + mod[4:5]) + mod[3:4]).astype(BF)
    vat_ref[0] = _dot_nt(wvat_ref[...], h).astype(BF)
    u = _dot(h, wa_ref[...])
    ka_ref[...] = u[:, :NA_WIDTH].astype(BF)
    ckv = (_rms(u[:, NA_WIDTH:NA_WIDTH + MLA_KV_RANK]) * gkv_ref[...]).astype(BF)
    vmt_ref[0] = _dot_nt(wuvt_ref[...], ckv).astype(BF)
    kn = _dot(ckv, wuk_ref[...])
    kr = _dot(h, wkr_ref[...])
    if latent:
        kr = kr * cosk_ref[...] + pltpu.roll(kr, LANES - MLA_ROPE, 1) * sink_ref[...]
    else:
        kr = jnp.where(lax.broadcasted_iota(jnp.int32, (1, LANES), 1) < MLA_QK, kr, 0.0)
    for hd in range(N_HEADS):
        sl = slice(hd * LANES, (hd + 1) * LANES)
        km_ref[:, sl] = (kn[:, sl] + kr).astype(BF)
    if latent:
        u = u[:, NA_WIDTH + MLA_KV_RANK:]
        qa_ref[...] = (u[:, :NA_WIDTH] * (NA_DIM ** -0.5 * LOG2E)).astype(BF)
        cq = u[:, NA_WIDTH:]
        q2 = _dot((_rms(cq) * gq_ref[...]).astype(BF), wuq_ref[...])
        cosq, sinq = cosq_ref[...], sinq_ref[...]
        for hd in range(N_HEADS):
            sl = slice(hd * LANES, (hd + 1) * LANES)
            qh = q2[:, sl]
            qm_ref[:, sl] = (qh * cosq + pltpu.roll(qh, LANES - MLA_ROPE, 1) * sinq).astype(BF)
        gates = jax.nn.sigmoid(_dot(h, wg_ref[...]) + bg_ref[...])
        ga_ref[...] = gates[:, :D_MODEL].astype(BF)
        gb_ref[...] = gates[:, D_MODEL:].astype(BF)


def _proj(x2d, mods, g, weights, tables, *, latent, mod_index, tm, seq):
    rows = x2d.shape[0]
    blocks = seq // tm
    row_spec = lambda w: pl.BlockSpec((tm, w), lambda i: (i, 0))
    col_spec = lambda w: pl.BlockSpec((1, w, tm), lambda i: (i // blocks, 0, i % blocks))
    row_shape = lambda w: jax.ShapeDtypeStruct((rows, w), BF)
    col_shape = lambda w: jax.ShapeDtypeStruct((rows // seq, w, seq), BF)
    ins = [x2d, mods, g] + list(weights) + list(tables)
    in_specs = ([row_spec(D_MODEL),
                 pl.BlockSpec((1, N_MOD, D_MODEL), lambda i: (mod_index(i), 0, 0)),
                 _const_spec(g.shape)] + [_const_spec(w.shape) for w in weights]
                + [pl.BlockSpec((tm, LANES), lambda i: (i % blocks, 0)) for _ in tables])
    outs = [(row_spec, row_shape, NA_WIDTH), (col_spec, col_shape, NA_WIDTH),
            (row_spec, row_shape, MLA_PAD), (col_spec, col_shape, MLA_VW)]
    if latent:
        outs += [(row_spec, row_shape, NA_WIDTH), (row_spec, row_shape, MLA_PAD),
                 (row_spec, row_shape, D_MODEL), (row_spec, row_shape, D_MODEL)]
    return pl.pallas_call(
        functools.partial(_proj_body, latent=latent),
        grid=(rows // tm,),
        in_specs=in_specs,
        out_specs=[spec(w) for spec, _, w in outs],
        out_shape=[shape(w) for _, shape, w in outs],
        compiler_params=_params(1),
        name="proj_lat" if latent else "proj_ctx",
    )(*ins)


def _na_geometry(case, i, j):
    if case == 0:
        return j < WIN_H, j - i
    if case == 1:
        return i <= j < i + WIN_H, j - i - WIN_H // 2
    return NA_KROWS - WIN_H <= j, j - i - (NA_KROWS - NA_QROWS)


def _na_bias_body(rpb_ref, o_ref):
    hd = pl.program_id(0)
    kc = lax.broadcasted_iota(jnp.int32, (GRID_W, GRID_W), 0)
    qc = lax.broadcasted_iota(jnp.int32, (GRID_W, GRID_W), 1)
    dcol = kc - qc
    ws = jnp.clip(qc - WIN_W // 2, 0, GRID_W - WIN_W)
    col_ok = (kc >= ws) & (kc < ws + WIN_W)
    n_dr, n_dc = 2 * WIN_H - 1, 2 * WIN_W - 1
    blocks = {}
    for case in range(NA_CASES):
        for i in range(NA_QROWS):
            for j in range(NA_KROWS):
                ok, dr = _na_geometry(case, i, j)
                blocks.setdefault(dr if ok else None, []).append((case, i, j))
    neg = jnp.full((GRID_W, GRID_W), NEG, F32)
    for dr, places in blocks.items():
        if dr is None:
            t = neg
        else:
            t = jnp.zeros((GRID_W, GRID_W), F32)
            base = hd * (n_dr * n_dc) + (dr + WIN_H - 1) * n_dc
            for d in range(n_dc):
                t = jnp.where(dcol == d - (WIN_W - 1), rpb_ref[base + d] * LOG2E, t)
            t = jnp.where(col_ok, t, NEG)
        for case, i, j in places:
            o_ref[case, 0, j * GRID_W:(j + 1) * GRID_W, i * GRID_W:(i + 1) * GRID_W] = t


def _na_bias(rpb):
    nq, nk = NA_QROWS * GRID_W, NA_KROWS * GRID_W
    return pl.pallas_call(
        _na_bias_body,
        grid=(N_HEADS,),
        in_specs=[pl.BlockSpec(memory_space=pltpu.SMEM)],
        out_specs=pl.BlockSpec((NA_CASES, 1, nk, nq), lambda h: (0, h, 0, 0)),
        out_shape=jax.ShapeDtypeStruct((NA_CASES, N_HEADS, nk, nq), F32),
        compiler_params=_params(1),
        name="na_bias",
    )(rpb.reshape(-1))


def _head_pipeline(scores, values, o_ref):
    def probs(s_list):
        m = functools.reduce(jnp.maximum, [jnp.max(s, axis=0, keepdims=True) for s in s_list])
        p_list = [jnp.exp2(s - m) for s in s_list]
        l = functools.reduce(lambda a, b: a + b, [jnp.sum(p, axis=0, keepdims=True) for p in p_list])
        return [p.astype(BF) for p in p_list], l

    top = lax.broadcasted_iota(jnp.int32, (LANES, 1), 0) < LANES // 2

    def finish(hd, p_list, l):
        res[hd] = values(hd, p_list) / l
        if hd % 2 == 1:
            sl = slice(hd // 2 * LANES, (hd // 2 + 1) * LANES)
            o_ref[0, :, sl] = jnp.where(top, res.pop(hd - 1), res.pop(hd)).T.astype(BF)

    res = {}
    s_cur, p_prev = scores(0), None
    for hd in range(N_HEADS):
        s_next = scores(hd + 1) if hd + 1 < N_HEADS else None
        p_cur = probs(s_cur)
        if p_prev is not None:
            finish(hd - 1, *p_prev)
        p_prev, s_cur = p_cur, s_next
    finish(N_HEADS - 1, *p_prev)


def _na_body(q_ref, k_ref, vt_ref, ck_ref, cvt_ref, tab_ref, o_ref):
    grp = pl.program_id(1)
    n_rows = k_ref.shape[1] // GRID_W
    k_row0 = jnp.clip(grp * NA_QROWS - WIN_H // 2, 0, n_rows - NA_KROWS)
    start = pl.multiple_of(k_row0 * GRID_W, NA_QROWS * GRID_W)
    nk = NA_KROWS * GRID_W
    low = lax.broadcasted_iota(jnp.int32, (1, LANES), 1) < NA_DIM

    def scores(hd):
        sl = slice(hd // 2 * LANES, (hd // 2 + 1) * LANES)
        q2 = q_ref[0, :, sl]
        qh = jnp.where(low if hd % 2 == 0 else ~low, q2, jnp.zeros_like(q2))
        return [_dot_nt(k_ref[0, pl.ds(start, nk), sl], qh) + tab_ref[0, hd], _dot_nt(ck_ref[0, :, sl], qh)]

    def values(hd, p_list):
        sl = slice(hd // 2 * LANES, (hd // 2 + 1) * LANES)
        return _dot(vt_ref[0, sl, pl.ds(start, nk)], p_list[0]) + _dot(cvt_ref[0, sl, :], p_list[1])

    _head_pipeline(scores, values, o_ref)


def _na(qa, ka, vat, cka, cvat, table, *, batch, seq):
    nq = NA_QROWS * GRID_W
    n_groups = seq // nq
    case = lambda g: (g + n_groups - 2) // (n_groups - 1) + g // (n_groups - 1)
    tile = pl.BlockSpec((1, nq, NA_WIDTH), lambda b, g: (b, g, 0))
    whole = lambda shape: pl.BlockSpec((1,) + shape, lambda b, g: (b, 0, 0))
    return pl.pallas_call(
        _na_body,
        grid=(batch, n_groups),
        in_specs=[tile, whole((seq, NA_WIDTH)), whole((NA_WIDTH, seq)),
                  whole((CTX_LEN, NA_WIDTH)), whole((NA_WIDTH, CTX_LEN)),
                  pl.BlockSpec((1, N_HEADS, NA_KROWS * GRID_W, nq), lambda b, g: (case(g), 0, 0, 0))],
        out_specs=tile,
        out_shape=jax.ShapeDtypeStruct((batch, seq, NA_WIDTH), BF),
        compiler_params=_params(2),
        name="na",
    )(qa, ka, vat, cka, cvat, table)


def _mla_body(q_ref, k_ref, ck_ref, vt_ref, cvt_ref, o_ref):
    def scores(hd):
        hs = slice(hd * LANES, (hd + 1) * LANES)
        qh = q_ref[0, :, hs]
        return [_dot_nt(ck_ref[0, :, hs], qh), _dot_nt(k_ref[0, :, hs], qh)]

    def values(hd, p_list):
        sl = slice(hd // 2 * LANES, (hd // 2 + 1) * LANES)
        return _dot(cvt_ref[0, sl, :], p_list[0]) + _dot(vt_ref[0, sl, :], p_list[1])

    _head_pipeline(scores, values, o_ref)


def _mla(qm, km, ckm, vmt, cvmt, *, batch, seq, tq):
    whole = lambda shape: pl.BlockSpec((1,) + shape, lambda b, i: (b, 0, 0))
    return pl.pallas_call(
        _mla_body,
        grid=(batch, seq // tq),
        in_specs=[pl.BlockSpec((1, tq, MLA_PAD), lambda b, i: (b, i, 0)),
                  whole((seq, MLA_PAD)), whole((CTX_LEN, MLA_PAD)),
                  whole((MLA_VW, seq)), whole((MLA_VW, CTX_LEN))],
        out_specs=pl.BlockSpec((1, tq, MLA_VW), lambda b, i: (b, i, 0)),
        out_shape=jax.ShapeDtypeStruct((batch, seq, MLA_VW), BF),
        compiler_params=_params(2),
        name="mla",
    )(qm, km, ckm, vmt, cvmt)


def _merge_body(x_ref, ona_ref, omla_ref, ga_ref, gb_ref, mod_ref, g_ref, wna_ref, wmla_ref, wout_ref, o_ref):
    mod = mod_ref[0]
    g = g_ref[...]
    y = (ga_ref[...].astype(F32) * _dot(ona_ref[...], wna_ref[...])
         + gb_ref[...].astype(F32) * _dot(omla_ref[...], wmla_ref[...]))
    y = _dot(y.astype(BF), wout_ref[...])
    o_ref[...] = x_ref[...] + mod[5:6] * (_rms(y) * g[3:4])


def _merge(x2d, ona, omla, ga, gb, mods, g, wna, wmla, wout, *, mod_index, tm):
    rows = x2d.shape[0]
    row_spec = lambda w: pl.BlockSpec((tm, w), lambda i: (i, 0))
    return pl.pallas_call(
        _merge_body,
        grid=(rows // tm,),
        in_specs=[row_spec(D_MODEL), row_spec(NA_WIDTH), row_spec(MLA_VW),
                  row_spec(D_MODEL), row_spec(D_MODEL),
                  pl.BlockSpec((1, N_MOD, D_MODEL), lambda i: (mod_index(i), 0, 0)),
                  _const_spec(g.shape), _const_spec(wna.shape), _const_spec(wmla.shape),
                  _const_spec(wout.shape)],
        out_specs=row_spec(D_MODEL),
        out_shape=jax.ShapeDtypeStruct(x2d.shape, F32),
        compiler_params=_params(1),
        name="merge",
    )(x2d, ona, omla, ga, gb, mods, g, wna, wmla, wout)


ROPE_SWAP = np.concatenate([np.arange(8, 16), np.arange(0, 8), np.arange(24, 32), np.arange(16, 24)])


def _rope_tables(seq):
    n_freq = MLA_ROPE // 4
    freqs = ROPE_BASE ** (-jnp.arange(n_freq, dtype=F32) / n_freq)
    t = jnp.arange(seq)
    ang_r = (t // GRID_W).astype(F32)[:, None] * freqs
    ang_c = (t % GRID_W).astype(F32)[:, None] * freqs
    cos = jnp.concatenate([jnp.cos(ang_r)] * 2 + [jnp.cos(ang_c)] * 2, axis=-1)
    sin = jnp.concatenate([-jnp.sin(ang_r), jnp.sin(ang_r), -jnp.sin(ang_c), jnp.sin(ang_c)], axis=-1)
    scale = MLA_QK ** -0.5 * LOG2E
    zeros = lambda w: jnp.zeros((seq, w), F32)
    tail = LANES - MLA_QK
    cosq = jnp.concatenate([jnp.full((seq, MLA_NOPE), scale, F32), scale * cos, zeros(tail)], axis=-1)
    sinq = jnp.concatenate([zeros(MLA_NOPE), scale * sin, zeros(tail)], axis=-1)
    cosk = jnp.concatenate([zeros(MLA_NOPE), cos, zeros(tail)], axis=-1)
    sink = jnp.concatenate([zeros(MLA_NOPE), sin, zeros(tail)], axis=-1)
    return cosq, sinq, cosk, sink


def _layer_weights(w_in, w_uq, w_ukv):
    o_cq = 3 * NA_WIDTH
    o_ckv = o_cq + MLA_Q_RANK
    o_kr = o_ckv + MLA_KV_RANK
    o_g = o_kr + MLA_ROPE
    w_qa, w_ka, w_va = w_in[:, :NA_WIDTH], w_in[:, NA_WIDTH:2 * NA_WIDTH], w_in[:, 2 * NA_WIDTH:o_cq]
    w_cq, w_ckv, w_kr = w_in[:, o_cq:o_ckv], w_in[:, o_ckv:o_kr], w_in[:, o_kr:o_g]
    wa_ctx = jnp.concatenate([w_ka, w_ckv], axis=1).astype(BF)
    wa_lat = jnp.concatenate([w_ka, w_ckv, w_qa, w_cq], axis=1).astype(BF)
    wvat = w_va.T.astype(BF)
    wkr = jnp.concatenate([jnp.zeros((D_MODEL, MLA_NOPE), F32), w_kr, w_kr[:, ROPE_SWAP]], axis=1).astype(BF)
    wg = w_in[:, o_g:].astype(BF)
    uq = w_uq.reshape(MLA_Q_RANK, N_HEADS, MLA_QK)
    wuq = jnp.concatenate([uq, uq[:, :, MLA_NOPE:][:, :, ROPE_SWAP]], axis=2).reshape(MLA_Q_RANK, MLA_PAD).astype(BF)
    ukv = w_ukv.reshape(MLA_KV_RANK, N_HEADS, MLA_NOPE + MLA_V)
    wuk = jnp.pad(ukv[:, :, :MLA_NOPE], ((0, 0), (0, 0), (0, LANES - MLA_NOPE))).reshape(MLA_KV_RANK, MLA_PAD)
    wuvt = ukv[:, :, MLA_NOPE:].reshape(MLA_KV_RANK, MLA_VW).T
    return wa_lat, wa_ctx, wvat, wkr, wg, wuq, wuk.astype(BF), wuvt.astype(BF)


def kernel(x, c, ctx, c_ctx, w_ada, b_ada, norm_g, ffn1_w1, ffn1_w3, ffn1_w2, w_in, b_gate, g_q_lora, g_kv_lora,
           w_uq, w_ukv, rpb, w_o_na, w_o_mla, w_out, ffn2_w1, ffn2_w3, ffn2_w2):
    batch, seq, d = x.shape
    n_ctx = ctx.shape[1]
    depth = w_ada.shape[0]
    n_rows = seq // GRID_W
    assert d == D_MODEL and n_ctx == CTX_LEN and seq % (NA_QROWS * GRID_W) == 0
    assert n_rows >= NA_KROWS and (n_rows - NA_KROWS) % NA_QROWS == 0 and (WIN_H // 2) % NA_QROWS == 0

    mod_rows = -(-(batch + 1) // 16) * 16
    ctx_mod = batch
    tm = 512
    lat_mod = lambda i: i // (seq // tm)
    ctx_mod_index = lambda i: ctx_mod
    tables = _rope_tables(seq)

    xl = x.reshape(batch * seq, d)
    xc = ctx.reshape(batch * n_ctx, d)
    for l in range(depth):
        last = l == depth - 1
        cc = jnp.zeros((mod_rows, d), F32).at[:batch].set(c).at[batch].set(c_ctx)
        mods = jnp.transpose(_ada(cc, w_ada[l], b_ada[l][None, :]), (1, 0, 2))
        g = norm_g[l]
        f1 = (ffn1_w1[l].astype(BF), ffn1_w3[l].astype(BF), ffn1_w2[l].astype(BF))
        f2 = (ffn2_w1[l].astype(BF), ffn2_w3[l].astype(BF), ffn2_w2[l].astype(BF))
        wa_lat, wa_ctx, wvat, wkr, wg, wuq, wuk, wuvt = _layer_weights(w_in[l], w_uq[l], w_ukv[l])
        gkv = g_kv_lora[l][None, :]

        xl = _ffn(xl, mods, g, *f1, km=0, kg=0, mod_index=lat_mod, tm=tm)
        xc = _ffn(xc, mods, g, *f1, km=0, kg=0, mod_index=ctx_mod_index, tm=tm)

        lat_w = (wa_lat, wvat, wkr, wuk, wuvt, gkv, wg, b_gate[l][None, :], g_q_lora[l][None, :], wuq)
        ka, vat, km_, vmt, qa, qm, ga, gb = _proj(xl, mods, g, lat_w, tables, latent=True, mod_index=lat_mod,
                                                  tm=tm, seq=seq)
        ctx_w = (wa_ctx, wvat, wkr, wuk, wuvt, gkv)
        cka, cvat, ckm, cvmt = _proj(xc, mods, g, ctx_w, (), latent=False, mod_index=ctx_mod_index,
                                     tm=n_ctx, seq=n_ctx)

        b3 = lambda t, n: t.reshape(batch, n, t.shape[-1])
        o_na = _na(b3(qa, seq), b3(ka, seq), vat, b3(cka, n_ctx), cvat, _na_bias(rpb[l]), batch=batch, seq=seq)
        o_mla = _mla(b3(qm, seq), b3(km_, seq), b3(ckm, n_ctx), vmt, cvmt, batch=batch, seq=seq, tq=512)
        xl = _merge(xl, o_na.reshape(batch * seq, -1), o_mla.reshape(batch * seq, -1), ga, gb, mods, g,
                    w_o_na[l].astype(BF), w_o_mla[l].astype(BF), w_out[l].astype(BF), mod_index=lat_mod, tm=tm)
        if not last:
            raise NotImplementedError("context-stream mixer update is only needed for depth > 1")
        xl = _ffn(xl, mods, g, *f2, km=6, kg=4, mod_index=lat_mod, tm=tm)
    return xl.reshape(batch, seq, d)
```
